```python
import math
import jax, jax.numpy as jnp
from jax import lax
import numpy as np

D_MODEL = 2048
BATCH = 4
SEQ = 8192
DEPTH = 4

RET_HEADS = 8
RET_DK = D_MODEL // RET_HEADS
RET_DV = D_MODEL // RET_HEADS
RET_W = RET_HEADS * RET_DV
RET_CHUNK = 128
SGU_GROUPS = 8
SGU_W = D_MODEL
SGU_GC = SGU_W // SGU_GROUPS
SGU_CHUNK = 128
D_FF = 5632
ROPE_BASE = 10000.0
EPS = 1e-6
N_BRANCH = 2
IN_SPLITS = (RET_HEADS * RET_DK, RET_HEADS * RET_DK, RET_W, RET_W, SGU_W, SGU_W, D_MODEL, D_MODEL)
IN_COLS = sum(IN_SPLITS)

kernel_name = "hybrid_retention_sgu_macaron"


def _rmsnorm(x, g):
    xf = x.astype(jnp.float32)
    y = xf * lax.rsqrt(jnp.mean(xf * xf, axis=-1, keepdims=True) + EPS)
    return (y * g.astype(jnp.float32)).astype(x.dtype)


def _layernorm(x, g, axis_size_last=True):
    xf = x.astype(jnp.float32)
    mu = jnp.mean(xf, axis=-1, keepdims=True)
    var = jnp.mean(jnp.square(xf - mu), axis=-1, keepdims=True)
    y = (xf - mu) * lax.rsqrt(var + EPS)
    return y * g.astype(jnp.float32)


def _swiglu(x, w_gu, w_down):
    a, g = jnp.split(x @ w_gu, 2, axis=-1)
    return (jax.nn.silu(g) * a) @ w_down


def _rotary(x, positions):
    half = x.shape[-1] // 2
    inv = ROPE_BASE ** (-jnp.arange(half, dtype=jnp.float32) / half)
    ang = positions.astype(jnp.float32)[..., None] * inv
    cos = jnp.cos(ang)[:, :, None, :].astype(x.dtype)
    sin = jnp.sin(ang)[:, :, None, :].astype(x.dtype)
    x1, x2 = x[..., :half], x[..., half:]
    return jnp.concatenate([x1 * cos - x2 * sin, x1 * sin + x2 * cos], axis=-1)


def _retention(q, k, v):
    B, S, H, DK = q.shape
    DV = v.shape[-1]
    nc = S // RET_CHUNK
    dt = q.dtype
    log_gamma = jnp.log1p(-jnp.exp2(-5.0 - jnp.arange(H, dtype=jnp.float32)))
    idx = jnp.arange(RET_CHUNK, dtype=jnp.float32)
    diff = idx[:, None] - idx[None, :]
    causal = diff >= 0
    dmask = jnp.where(causal[None], jnp.exp(jnp.where(causal, diff, 0.0)[None] * log_gamma[:, None, None]), 0.0).astype(dt)
    xi = jnp.exp((idx[:, None] + 1.0) * log_gamma[None]).astype(dt)
    zeta = jnp.exp((RET_CHUNK - 1.0 - idx)[:, None] * log_gamma[None]).astype(dt)
    gamma_c = jnp.exp(RET_CHUNK * log_gamma).astype(dt)

    k = k * jnp.asarray(DK ** -0.5, dt)
    qc = q.reshape(B, nc, RET_CHUNK, H, DK)
    kc = k.reshape(B, nc, RET_CHUNK, H, DK)
    vc = v.reshape(B, nc, RET_CHUNK, H, DV)
    scores = jnp.einsum('bnihd,bnjhd->bnhij', qc, kc) * dmask
    intra = jnp.einsum('bnhij,bnjhe->bnihe', scores, vc)

    def step(R, inp):
        q_i, k_i, v_i = inp
        cross = jnp.einsum('bihd,bhde->bihe', q_i, R) * xi[None, :, :, None]
        R = R * gamma_c[None, :, None, None] + jnp.einsum('bjhd,bjhe->bhde', k_i * zeta[None, :, :, None], v_i)
        return R, cross

    R0 = jnp.zeros((B, H, DK, DV), dt)
    _, cross = lax.scan(step, R0, (jnp.moveaxis(qc, 1, 0), jnp.moveaxis(kc, 1, 0), jnp.moveaxis(vc, 1, 0)))
    cross = jnp.moveaxis(cross, 0, 1)
    return (intra + cross).reshape(B, S, H, DV)


def _spatial_gating(u, v, ln_g, w_s, b_s):
    B, S, _ = v.shape
    nc = S // SGU_CHUNK
    u = jax.nn.gelu(u, approximate=False)
    v = _layernorm(jax.nn.gelu(v, approximate=False), ln_g).astype(u.dtype)
    vg = v.reshape(B, nc, SGU_CHUNK, SGU_GROUPS, SGU_GC)
    tril = jnp.tril(jnp.ones((SGU_CHUNK, SGU_CHUNK), dtype=bool))
    w_m = jnp.where(tril[None], w_s, jnp.zeros((), w_s.dtype))
    mixed = jnp.einsum('gij,bnjgc->bnigc', w_m, vg) + b_s.T[None, None, :, :, None]
    return u * mixed.reshape(B, S, SGU_W)


def _layer(x, positions, ffn1_norm, ffn1_w_gu, ffn1_w_down, mix_norm, w_in, b_gate, ret_gn,
           sgu_ln, sgu_w, sgu_b, w_branch_ret, w_branch_sgu, w_out, ffn2_norm, ffn2_w_gu, ffn2_w_down):
    B, S, D = x.shape
    x = x + 0.5 * _swiglu(_rmsnorm(x, ffn1_norm), ffn1_w_gu, ffn1_w_down)
    h = _rmsnorm(x, mix_norm)
    proj = h @ w_in
    offs = list(np.cumsum(IN_SPLITS)[:-1])
    q, k, v, g_ret, u_s, v_s, gate_ret, gate_sgu = jnp.split(proj, offs, axis=-1)
    q = _rotary(q.reshape(B, S, RET_HEADS, RET_DK), positions)
    k = _rotary(k.reshape(B, S, RET_HEADS, RET_DK), positions)
    v = v.reshape(B, S, RET_HEADS, RET_DV)
    y_ret = _retention(q, k, v)
    y_ret = _layernorm(y_ret, ret_gn.reshape(RET_HEADS, RET_DV)).astype(x.dtype).reshape(B, S, RET_W)
    y_ret = jax.nn.silu(g_ret) * y_ret
    y_sgu = _spatial_gating(u_s, v_s, sgu_ln, sgu_w, sgu_b)
    gates = jax.nn.sigmoid(jnp.concatenate([gate_ret, gate_sgu], axis=-1) + b_gate)
    ga, gb = jnp.split(gates, 2, axis=-1)
    merged = ga * (y_ret @ w_branch_ret) + gb * (y_sgu @ w_branch_sgu)
    x = x + merged @ w_out
    x = x + 0.5 * _swiglu(_rmsnorm(x, ffn2_norm), ffn2_w_gu, ffn2_w_down)
    return x


def setup_inputs(seed: int = 0) -> dict:
    key = jax.random.key(seed)
    ks = jax.random.split(key, 20)
    L, D = DEPTH, D_MODEL

    def nrm(k, shape, scale):
        return jax.random.normal(k, shape, jnp.float32) * scale

    def gain(k, shape):
        return 1.0 + 0.02 * jax.random.normal(k, shape, jnp.float32)

    x = jax.random.normal(ks[0], (BATCH, SEQ, D), jnp.float32)
    positions = jnp.tile(jnp.arange(SEQ, dtype=jnp.int32)[None, :], (BATCH, 1))
    return {
        "x": x,
        "positions": positions,
        "ffn1_norm": gain(ks[1], (L, D)),
        "ffn1_w_gu": nrm(ks[2], (L, D, 2 * D_FF), D ** -0.5),
        "ffn1_w_down": nrm(ks[3], (L, D_FF, D), D_FF ** -0.5),
        "mix_norm": gain(ks[4], (L, D)),
        "w_in": nrm(ks[5], (L, D, IN_COLS), D ** -0.5),
        "b_gate": nrm(ks[6], (L, N_BRANCH * D), 0.02),
        "ret_gn": gain(ks[7], (L, RET_W)),
        "sgu_ln": gain(ks[8], (L, SGU_W)),
        "sgu_w": nrm(ks[9], (L, SGU_GROUPS, SGU_CHUNK, SGU_CHUNK), SGU_CHUNK ** -0.5),
        "sgu_b": gain(ks[10], (L, SGU_GROUPS, SGU_CHUNK)),
        "w_branch_ret": nrm(ks[11], (L, RET_W, D), RET_W ** -0.5),
        "w_branch_sgu": nrm(ks[12], (L, SGU_W, D), SGU_W ** -0.5),
        "w_out": nrm(ks[13], (L, D, D), D ** -0.5),
        "ffn2_norm": gain(ks[14], (L, D)),
        "ffn2_w_gu": nrm(ks[15], (L, D, 2 * D_FF), D ** -0.5),
        "ffn2_w_down": nrm(ks[16], (L, D_FF, D), D_FF ** -0.5),
        "final_norm": gain(ks[17], (D,)),
    }


def reference(x, positions, ffn1_norm, ffn1_w_gu, ffn1_w_down, mix_norm, w_in, b_gate, ret_gn,
              sgu_ln, sgu_w, sgu_b, w_branch_ret, w_branch_sgu, w_out, ffn2_norm, ffn2_w_gu,
              ffn2_w_down, final_norm):
    for l in range(DEPTH):
        x = _layer(x, positions, ffn1_norm[l], ffn1_w_gu[l], ffn1_w_down[l], mix_norm[l], w_in[l],
                   b_gate[l], ret_gn[l], sgu_ln[l], sgu_w[l], sgu_b[l], w_branch_ret[l],
                   w_branch_sgu[l], w_out[l], ffn2_norm[l], ffn2_w_gu[l], ffn2_w_down[l])
    return _rmsnorm(x, final_norm)
```

```python
import functools
import math

import jax
import jax.numpy as jnp
from jax import lax
from jax.experimental import pallas as pl
from jax.experimental.pallas import tpu as pltpu

F32 = jnp.float32
BF16 = jnp.bfloat16

EPS = 1e-6
ROPE_BASE = 10000.0
N_HEADS = 8
HEAD_DIM = 256
CHUNK = 128
N_GROUPS = 8
N_SEG = 8
LANES = 128
VMEM_LIMIT = 56 * 1024 * 1024


def _params(sem, vmem=VMEM_LIMIT):
    return pltpu.CompilerParams(dimension_semantics=sem, vmem_limit_bytes=vmem)


def _rms(x, g):
    return x * lax.rsqrt(jnp.mean(x * x, axis=-1, keepdims=True) + EPS) * g


def _gelu(x):
    return 0.5 * x * (1.0 + lax.erf(x * (2.0 ** -0.5)))


def _layernorm(x, g):
    mu = jnp.mean(x, axis=-1, keepdims=True)
    xc = x - mu
    var = jnp.mean(xc * xc, axis=-1, keepdims=True)
    return xc * lax.rsqrt(var + EPS) * g


def _rope_kernel(pos_ref, inv_ref, cos_ref, sin_ref):
    ang = pos_ref[...] * inv_ref[...]
    cos_ref[...] = jnp.cos(ang)
    sin_ref[...] = jnp.sin(ang)


def rope_tables(pos, inv, *, tr=1024):
    t = pos.shape[0]
    half = inv.shape[1]
    return pl.pallas_call(
        _rope_kernel,
        grid=(t // tr,),
        in_specs=[pl.BlockSpec((tr, 1), lambda i: (i, 0)),
                  pl.BlockSpec((1, half), lambda i: (0, 0))],
        out_specs=[pl.BlockSpec((tr, half), lambda i: (i, 0))] * 2,
        out_shape=[jax.ShapeDtypeStruct((t, half), F32)] * 2,
        compiler_params=_params(("parallel",)),
        name="rope_tables",
    )(pos, inv)


def _ffn_kernel(x_ref, nw_ref, wa_ref, wg_ref, wd_ref, nxt_ref, o_ref, h2_ref, h_scr, acc_scr,
                *, n_f, h2_dtype):
    f = pl.program_id(1)

    @pl.when(f == 0)
    def _():
        h_scr[...] = _rms(x_ref[...], nw_ref[...]).astype(BF16)
        acc_scr[...] = jnp.zeros_like(acc_scr)

    h = h_scr[...]
    a = jnp.dot(h, wa_ref[...], preferred_element_type=F32)
    g = jnp.dot(h, wg_ref[...], preferred_element_type=F32)
    p = (jax.nn.silu(g) * a).astype(BF16)
    acc_scr[...] += jnp.dot(p, wd_ref[...], preferred_element_type=F32)

    @pl.when(f == n_f - 1)
    def _():
        xn = x_ref[...] + 0.5 * acc_scr[...]
        o_ref[...] = xn
        h2_ref[...] = _rms(xn, nxt_ref[...]).astype(h2_dtype)


def ffn(x, nw, w_gu, w_down, nxt, *, h2_dtype, tm=512, tf=512):
    t, d = x.shape
    d_ff = w_down.shape[0]
    n_f = d_ff // tf
    kern = functools.partial(_ffn_kernel, n_f=n_f, h2_dtype=h2_dtype)
    return pl.pallas_call(
        kern,
        grid=(t // tm, n_f),
        in_specs=[pl.BlockSpec((tm, d), lambda i, f: (i, 0)),
                  pl.BlockSpec((1, d), lambda i, f: (0, 0)),
                  pl.BlockSpec((d, tf), lambda i, f: (0, f)),
                  pl.BlockSpec((d, tf), lambda i, f: (0, n_f + f)),
                  pl.BlockSpec((tf, d), lambda i, f: (f, 0)),
                  pl.BlockSpec((1, d), lambda i, f: (0, 0))],
        out_specs=[pl.BlockSpec((tm, d), lambda i, f: (i, 0)),
                   pl.BlockSpec((tm, d), lambda i, f: (i, 0))],
        out_shape=[jax.ShapeDtypeStruct((t, d), F32),
                   jax.ShapeDtypeStruct((t, d), h2_dtype)],
        scratch_shapes=[pltpu.VMEM((tm, d), BF16), pltpu.VMEM((tm, d), F32)],
        compiler_params=_params(("parallel", "arbitrary")),
        name="ffn",
    )(x, nw, w_gu, w_gu, w_down, nxt)


def _rotary_store(r, cos, sin, o_ref, scale):
    half = HEAD_DIM // 2
    for hd in range(N_HEADS):
        lo = hd * HEAD_DIM
        x1 = r[:, lo:lo + half]
        x2 = r[:, lo + half:lo + HEAD_DIM]
        o1 = x1 * cos - x2 * sin
        o2 = x1 * sin + x2 * cos
        if scale != 1.0:
            o1 = o1 * scale
            o2 = o2 * scale
        o_ref[:, lo:lo + half] = o1.astype(o_ref.dtype)
        o_ref[:, lo + half:lo + HEAD_DIM] = o2.astype(o_ref.dtype)


def _proj_kernel(h_ref, w_ref, cos_ref, sin_ref, bg_ref, ln_ref, o_ref):
    seg = pl.program_id(0)
    r = jnp.dot(h_ref[...], w_ref[...], preferred_element_type=F32)

    @pl.when(seg == 0)
    def _():
        _rotary_store(r, cos_ref[...], sin_ref[...], o_ref, 1.0)

    @pl.when(seg == 1)
    def _():
        _rotary_store(r, cos_ref[...], sin_ref[...], o_ref, HEAD_DIM ** -0.5)

    @pl.when(seg == 2)
    def _():
        o_ref[...] = r.astype(o_ref.dtype)

    @pl.when(seg == 3)
    def _():
        o_ref[...] = jax.nn.silu(r).astype(o_ref.dtype)

    @pl.when(seg == 4)
    def _():
        o_ref[...] = _gelu(r).astype(o_ref.dtype)

    @pl.when(seg == 5)
    def _():
        o_ref[...] = _layernorm(_gelu(r), ln_ref[...]).astype(o_ref.dtype)

    @pl.when(seg >= 6)
    def _():
        o_ref[...] = jax.nn.sigmoid(r + bg_ref[...]).astype(o_ref.dtype)


def proj(h, w_in, cos, sin, b_gate, sgu_ln, *, tm=512):
    t, d = h.shape
    seg_w = w_in.shape[1] // N_SEG
    half = cos.shape[1]
    return pl.pallas_call(
        _proj_kernel,
        grid=(N_SEG, t // tm),
        in_specs=[pl.BlockSpec((tm, d), lambda s, i: (i, 0)),
                  pl.BlockSpec((d, seg_w), lambda s, i: (0, s)),
                  pl.BlockSpec((tm, half), lambda s, i: (i, 0)),
                  pl.BlockSpec((tm, half), lambda s, i: (i, 0)),
                  pl.BlockSpec((1, seg_w), lambda s, i: (0, jnp.maximum(s - 6, 0))),
                  pl.BlockSpec((1, seg_w), lambda s, i: (0, 0))],
        out_specs=pl.BlockSpec((tm, seg_w), lambda s, i: (i, s)),
        out_shape=jax.ShapeDtypeStruct((t, w_in.shape[1]), BF16),
        compiler_params=_params(("arbitrary", "arbitrary")),
        name="proj",
    )(h, w_in, cos, sin, b_gate, sgu_ln)


def _ret_kernel(lg_ref, q_ref, k_ref, v_ref, sg_ref, gn_ref, o_ref, r_scr, *, n_chunks):
    c_blk = pl.program_id(2)

    @pl.when(c_blk == 0)
    def _():
        r_scr[...] = jnp.zeros_like(r_scr)

    lg = lg_ref[0][0:1, :]
    row = lax.broadcasted_iota(jnp.int32, (CHUNK, CHUNK), 0)
    col = lax.broadcasted_iota(jnp.int32, (CHUNK, CHUNK), 1)
    diff = (row - col).astype(F32)
    causal = row >= col
    dmask = jnp.where(causal, jnp.exp(jnp.where(causal, diff, 0.0) * lg), 0.0)
    idx = lax.broadcasted_iota(jnp.int32, (CHUNK, LANES), 0).astype(F32)
    xi = jnp.exp((idx + 1.0) * lg)
    zeta = jnp.exp((CHUNK - 1.0 - idx) * lg)
    gamma_c = jnp.exp(CHUNK * lg)
    xi2 = jnp.concatenate([xi, xi], axis=1)
    zeta2 = jnp.concatenate([zeta, zeta], axis=1)
    gamma2 = jnp.concatenate([gamma_c, gamma_c], axis=1)
    gn = gn_ref[...]

    state = r_scr[...]
    for c in range(n_chunks):
        rows = pl.ds(c * CHUNK, CHUNK)
        q = q_ref[rows, :]
        k = k_ref[rows, :]
        v = v_ref[rows, :]
        s = lax.dot_general(q, k, (((1,), (1,)), ((), ())), preferred_element_type=F32) * dmask
        intra = jnp.dot(s.astype(BF16), v, preferred_element_type=F32)
        cross = jnp.dot(q, state.astype(BF16), preferred_element_type=F32) * xi2
        kz = (k.astype(F32) * zeta2).astype(BF16)
        kv = lax.dot_general(kz, v, (((0,), (0,)), ((), ())), preferred_element_type=F32)
        state = state * gamma2 + kv
        y = _layernorm(intra + cross, gn)
        o_ref[rows, :] = (sg_ref[rows, :].astype(F32) * y).astype(o_ref.dtype)
    r_scr[...] = state


def retention(pj, lg_tab, ret_gn, *, batch, rows=1024):
    t = pj.shape[0]
    seq = t // batch
    n_blk = seq // rows
    kern = functools.partial(_ret_kernel, n_chunks=rows // CHUNK)

    def col(off):
        return lambda b, h, c: (b * n_blk + c, off + h)

    blk = (rows, HEAD_DIM)
    return pl.pallas_call(
        kern,
        grid=(batch, N_HEADS, n_blk),
        in_specs=[pl.BlockSpec((1, 8, LANES), lambda b, h, c: (h, 0, 0)),
                  pl.BlockSpec(blk, col(0)),
                  pl.BlockSpec(blk, col(N_HEADS)),
                  pl.BlockSpec(blk, col(2 * N_HEADS)),
                  pl.BlockSpec(blk, col(3 * N_HEADS)),
                  pl.BlockSpec((1, HEAD_DIM), lambda b, h, c: (0, h))],
        out_specs=pl.BlockSpec(blk, lambda b, h, c: (b * n_blk + c, h)),
        out_shape=jax.ShapeDtypeStruct((t, N_HEADS * HEAD_DIM), BF16),
        scratch_shapes=[pltpu.VMEM((HEAD_DIM, HEAD_DIM), F32)],
        compiler_params=_params(("parallel", "parallel", "arbitrary")),
        name="retention",
    )(lg_tab, pj, pj, pj, pj, ret_gn)


def _sgu_kernel(u_ref, v_ref, w_ref, b_ref, o_ref, *, n_chunks):
    row = lax.broadcasted_iota(jnp.int32, (CHUNK, CHUNK), 0)
    col = lax.broadcasted_iota(jnp.int32, (CHUNK, CHUNK), 1)
    causal = row >= col
    gc = o_ref.shape[1] // N_GROUPS
    for g in range(N_GROUPS):
        w_m = jnp.where(causal, w_ref[g], 0.0).astype(BF16)
        bias = b_ref[g]
        cols = slice(g * gc, (g + 1) * gc)
        for c in range(n_chunks):
            rows = pl.ds(c * CHUNK, CHUNK)
            mixed = jnp.dot(w_m, v_ref[rows, cols], preferred_element_type=F32) + bias
            o_ref[rows, cols] = (u_ref[rows, cols].astype(F32) * mixed).astype(o_ref.dtype)


def spatial_gating(pj, sgu_w, sgu_b, *, rows=512):
    t = pj.shape[0]
    d = pj.shape[1] // N_SEG
    kern = functools.partial(_sgu_kernel, n_chunks=rows // CHUNK)
    return pl.pallas_call(
        kern,
        grid=(t // rows,),
        in_specs=[pl.BlockSpec((rows, d), lambda i: (i, 4)),
                  pl.BlockSpec((rows, d), lambda i: (i, 5)),
                  pl.BlockSpec((N_GROUPS, CHUNK, CHUNK), lambda i: (0, 0, 0)),
                  pl.BlockSpec((N_GROUPS, CHUNK, 1), lambda i: (0, 0, 0))],
        out_specs=pl.BlockSpec((rows, d), lambda i: (i, 0)),
        out_shape=jax.ShapeDtypeStruct((t, d), BF16),
        compiler_params=_params(("parallel",)),
        name="spatial_gating",
    )(pj, pj, sgu_w, sgu_b)


def _merge_kernel(x_ref, yr_ref, ys_ref, ga_ref, gb_ref, wr_ref, ws_ref, wo_ref, o_ref):
    a = jnp.dot(yr_ref[...], wr_ref[...], preferred_element_type=F32)
    b = jnp.dot(ys_ref[...], ws_ref[...], preferred_element_type=F32)
    merged = (ga_ref[...].astype(F32) * a + gb_ref[...].astype(F32) * b).astype(BF16)
    o_ref[...] = x_ref[...] + jnp.dot(merged, wo_ref[...], preferred_element_type=F32)


def merge(x, y_ret, y_sgu, pj, w_ret, w_sgu, w_out, *, tm=256):
    t, d = x.shape
    tile = lambda j: pl.BlockSpec((tm, d), lambda i: (i, j))
    resident = pl.BlockSpec((d, d), lambda i: (0, 0), pipeline_mode=pl.Buffered(1))
    return pl.pallas_call(
        _merge_kernel,
        grid=(t // tm,),
        in_specs=[tile(0), tile(0), tile(0), tile(6), tile(7), resident, resident, resident],
        out_specs=tile(0),
        out_shape=jax.ShapeDtypeStruct((t, d), F32),
        compiler_params=_params(("parallel",)),
        name="merge",
    )(x, y_ret, y_sgu, pj, pj, w_ret, w_sgu, w_out)


def kernel(x, positions, ffn1_norm, ffn1_w_gu, ffn1_w_down, mix_norm, w_in, b_gate, ret_gn,
           sgu_ln, sgu_w, sgu_b, w_branch_ret, w_branch_sgu, w_out, ffn2_norm, ffn2_w_gu,
           ffn2_w_down, final_norm):
    batch, seq, d = x.shape
    depth = ffn1_norm.shape[0]
    t = batch * seq
    half = HEAD_DIM // 2

    xs = x.reshape(t, d)
    pos = positions.reshape(t, 1).astype(F32)
    inv = (ROPE_BASE ** (-jnp.arange(half, dtype=F32) / half)).reshape(1, half)
    cos, sin = rope_tables(pos, inv)

    log_gamma = jnp.log1p(-jnp.exp2(-5.0 - jnp.arange(N_HEADS, dtype=F32)))
    lg_tab = jnp.broadcast_to(log_gamma[:, None, None], (N_HEADS, 8, LANES))

    row = lambda a: a.reshape(1, -1)
    for l in range(depth):
        xs, h = ffn(xs, row(ffn1_norm[l]), ffn1_w_gu[l].astype(BF16), ffn1_w_down[l].astype(BF16),
                    row(mix_norm[l]), h2_dtype=BF16)
        pj = proj(h, w_in[l].astype(BF16), cos, sin, row(b_gate[l]), row(sgu_ln[l]))
        y_ret = retention(pj, lg_tab, row(ret_gn[l]), batch=batch)
        y_sgu = spatial_gating(pj, sgu_w[l], sgu_b[l].reshape(N_GROUPS, CHUNK, 1))
        xs = merge(xs, y_ret, y_sgu, pj, w_branch_ret[l].astype(BF16),
                   w_branch_sgu[l].astype(BF16), w_out[l].astype(BF16))
        xs, out = ffn(xs, row(ffn2_norm[l]), ffn2_w_gu[l].astype(BF16), ffn2_w_down[l].astype(BF16),
                      row(final_norm), h2_dtype=F32)
    return out.reshape(batch, seq, d)
```

```python
import functools

import jax
import jax.numpy as jnp
from jax import lax
from jax.experimental import pallas as pl
from jax.experimental.pallas import tpu as pltpu

F32 = jnp.float32
BF16 = jnp.bfloat16

EPS = 1e-6
ROPE_BASE = 10000.0
N_HEADS = 8
HEAD_DIM = 256
CHUNK = 128
N_GROUPS = 8
N_SEG = 8
LANES = 128
MXU_DIM = 256
VMEM_LIMIT = 58 * 1024 * 1024


def _params(sem):
    return pltpu.CompilerParams(dimension_semantics=sem, vmem_limit_bytes=VMEM_LIMIT)


def _rms(x, g):
    return x * lax.rsqrt(jnp.mean(x * x, axis=-1, keepdims=True) + EPS) * g


def _gelu(x):
    return 0.5 * x * (1.0 + lax.erf(x * (2.0 ** -0.5)))


def _layernorm(x, g):
    mu = jnp.mean(x, axis=-1, keepdims=True)
    xc = x - mu
    var = jnp.mean(xc * xc, axis=-1, keepdims=True)
    return xc * lax.rsqrt(var + EPS) * g


def _resident(shape, index_map):
    return pl.BlockSpec(shape, index_map, pipeline_mode=pl.Buffered(1))


def _rope_kernel(pos_ref, inv_ref, cos_ref, sin_ref):
    ang = pos_ref[...] * inv_ref[...]
    cos_ref[...] = jnp.cos(ang)
    sin_ref[...] = jnp.sin(ang)


def rope_tables(pos, inv, *, tr=1024):
    t = pos.shape[0]
    half = inv.shape[1]
    return pl.pallas_call(
        _rope_kernel,
        grid=(t // tr,),
        in_specs=[pl.BlockSpec((tr, 1), lambda i: (i, 0)),
                  pl.BlockSpec((1, half), lambda i: (0, 0))],
        out_specs=[pl.BlockSpec((tr, half), lambda i: (i, 0))] * 2,
        out_shape=[jax.ShapeDtypeStruct((t, half), F32)] * 2,
        compiler_params=_params(("parallel",)),
        name="rope_tables",
    )(pos, inv)


def _rms_kernel(x_ref, g_ref, o_ref):
    o_ref[...] = _rms(x_ref[...], g_ref[...]).astype(o_ref.dtype)


def rmsnorm(x, g, *, tm=512):
    t, d = x.shape
    return pl.pallas_call(
        _rms_kernel,
        grid=(t // tm,),
        in_specs=[pl.BlockSpec((tm, d), lambda i: (i, 0)),
                  pl.BlockSpec((1, d), lambda i: (0, 0))],
        out_specs=pl.BlockSpec((tm, d), lambda i: (i, 0)),
        out_shape=jax.ShapeDtypeStruct((t, d), BF16),
        compiler_params=_params(("parallel",)),
        name="rmsnorm",
    )(x, g)


def _ffn_kernel(*refs, chunks, emit_x, emit_h):
    x_ref, h_ref, wa_ref, wg_ref, wd_ref = refs[:5]
    rest = list(refs[5:])
    nxt_ref = rest.pop(0) if emit_h else None
    o_ref = rest.pop(0) if emit_x else None
    h2_ref = rest.pop(0) if emit_h else None

    h = h_ref[...]
    acc = x_ref[...]
    off = 0
    for cw in chunks:
        a = jnp.dot(h, wa_ref[:, off:off + cw], preferred_element_type=F32)
        g = jnp.dot(h, wg_ref[:, off:off + cw], preferred_element_type=F32)
        p = (0.5 * jax.nn.silu(g) * a).astype(BF16)
        acc = acc + jnp.dot(p, wd_ref[off:off + cw, :], preferred_element_type=F32)
        off += cw
    if emit_x:
        o_ref[...] = acc
    if emit_h:
        h2_ref[...] = _rms(acc, nxt_ref[...]).astype(h2_ref.dtype)


def ffn_pass(x, h, w_gu, w_down, layer, slab, n_slabs, nxt=None, *, emit_x=True, h2_dtype=BF16, tm=256):
    t, d = x.shape
    d_ff = w_down.shape[1]
    ts = d_ff // n_slabs
    chunks = [2 * MXU_DIM] * (ts // (2 * MXU_DIM))
    if ts % (2 * MXU_DIM):
        chunks.append(ts % (2 * MXU_DIM))
    emit_h = nxt is not None
    kern = functools.partial(_ffn_kernel, chunks=tuple(chunks), emit_x=emit_x, emit_h=emit_h)
    tile = pl.BlockSpec((tm, d), lambda i: (i, 0))
    in_specs = [tile, tile,
                _resident((None, d, ts), lambda i: (layer, 0, slab)),
                _resident((None, d, ts), lambda i: (layer, 0, n_slabs + slab)),
                _resident((None, ts, d), lambda i: (layer, slab, 0))]
    args = [x, h, w_gu, w_gu, w_down]
    out_specs, out_shape = [], []
    if emit_h:
        in_specs.append(pl.BlockSpec((1, d), lambda i: (0, 0)))
        args.append(nxt)
    if emit_x:
        out_specs.append(tile)
        out_shape.append(jax.ShapeDtypeStruct((t, d), F32))
    if emit_h:
        out_specs.append(tile)
        out_shape.append(jax.ShapeDtypeStruct((t, d), h2_dtype))
    return pl.pallas_call(
        kern,
        grid=(t // tm,),
        in_specs=in_specs,
        out_specs=out_specs,
        out_shape=out_shape,
        compiler_params=_params(("parallel",)),
        name="ffn_pass",
    )(*args)


def ffn(x, h, w_gu, w_down, layer, nxt, *, emit_x=True, h2_dtype=BF16, n_slabs=2):
    for slab in range(n_slabs - 1):
        (x,) = ffn_pass(x, h, w_gu, w_down, layer, slab, n_slabs)
    return ffn_pass(x, h, w_gu, w_down, layer, n_slabs - 1, n_slabs, nxt, emit_x=emit_x, h2_dtype=h2_dtype)


def _proj_kernel(*refs, kind, ncol):
    h_ref, w_ref = refs[:2]
    o_ref = refs[-1]
    extra = refs[2:-1]
    h = h_ref[...]
    width = o_ref.shape[1]
    cw = width // ncol

    def cols(n):
        return slice(n * cw, (n + 1) * cw)

    if kind == "lngelu":
        (ln_ref,) = extra
        total = jnp.zeros((h.shape[0], 1), F32)
        parts = []
        for n in range(ncol):
            gl = _gelu(jnp.dot(h, w_ref[:, cols(n)], preferred_element_type=F32))
            total = total + jnp.sum(gl, axis=-1, keepdims=True)
            parts.append(gl)
        mu = total * (1.0 / width)
        ssq = jnp.zeros_like(total)
        for n in range(ncol):
            parts[n] = parts[n] - mu
            ssq = ssq + jnp.sum(parts[n] * parts[n], axis=-1, keepdims=True)
        rstd = lax.rsqrt(ssq * (1.0 / width) + EPS)
        for n in range(ncol):
            o_ref[:, cols(n)] = (parts[n] * rstd * ln_ref[:, cols(n)]).astype(o_ref.dtype)
        return

    if kind == "rotary":
        cos_ref, sin_ref = extra
        cos, sin = cos_ref[...], sin_ref[...]
        scale = jnp.where(pl.program_id(0) == 0, 1.0, HEAD_DIM ** -0.5).astype(F32)
        cos, sin = cos * scale, sin * scale
        half = HEAD_DIM // 2
    for n in range(ncol):
        r = jnp.dot(h, w_ref[:, cols(n)], preferred_element_type=F32)
        if kind == "rotary":
            for hd in range(cw // HEAD_DIM):
                lo = n * cw + hd * HEAD_DIM
                x1 = r[:, hd * HEAD_DIM:hd * HEAD_DIM + half]
                x2 = r[:, hd * HEAD_DIM + half:(hd + 1) * HEAD_DIM]
                o_ref[:, lo:lo + half] = (x1 * cos - x2 * sin).astype(o_ref.dtype)
                o_ref[:, lo + half:lo + HEAD_DIM] = (x1 * sin + x2 * cos).astype(o_ref.dtype)
            continue
        if kind == "silu":
            r = jax.nn.silu(r)
        elif kind == "gelu":
            r = _gelu(r)
        elif kind == "gate":
            r = jax.nn.sigmoid(r + extra[0][:, cols(n)])
        o_ref[:, cols(n)] = r.astype(o_ref.dtype)


def proj(h, w_in, layer, seg0, nseg, kind, extra=(), extra_specs=(), *, tm=512, ncol=4):
    t, d = h.shape
    seg_w = w_in.shape[2] // N_SEG
    kern = functools.partial(_proj_kernel, kind=kind, ncol=ncol)
    return pl.pallas_call(
        kern,
        grid=(nseg, t // tm),
        in_specs=[pl.BlockSpec((tm, d), lambda s, i: (i, 0)),
                  pl.BlockSpec((None, d, seg_w), lambda s, i: (layer, 0, seg0 + s)),
                  *extra_specs],
        out_specs=pl.BlockSpec((tm, seg_w), lambda s, i: (i, s)),
        out_shape=jax.ShapeDtypeStruct((t, nseg * seg_w), BF16),
        compiler_params=_params(("arbitrary", "arbitrary")),
        name="proj_" + kind,
    )(h, w_in, *extra)


def _ret_kernel(lg_ref, q_ref, k_ref, v_ref, sg_ref, gn_ref, o_ref, r_scr, *, n_chunks):
    c_blk = pl.program_id(2)

    @pl.when(c_blk == 0)
    def _():
        r_scr[...] = jnp.zeros_like(r_scr)

    lg = lg_ref[0][0:1, :]
    row = lax.broadcasted_iota(jnp.int32, (CHUNK, CHUNK), 0)
    col = lax.broadcasted_iota(jnp.int32, (CHUNK, CHUNK), 1)
    diff = (row - col).astype(F32)
    causal = row >= col
    dmask = jnp.where(causal, jnp.exp(jnp.where(causal, diff, 0.0) * lg), 0.0)
    idx = lax.broadcasted_iota(jnp.int32, (CHUNK, LANES), 0).astype(F32)
    xi = jnp.exp((idx + 1.0) * lg)
    zeta = jnp.exp((CHUNK - 1.0 - idx) * lg)
    gamma_c = jnp.exp(CHUNK * lg)
    xi2 = jnp.concatenate([xi, xi], axis=1)
    zeta2 = jnp.concatenate([zeta, zeta], axis=1)
    gamma2 = jnp.concatenate([gamma_c, gamma_c], axis=1)
    gn = gn_ref[...]

    state = r_scr[...]
    for c in range(n_chunks):
        rows = pl.ds(c * CHUNK, CHUNK)
        q = q_ref[rows, :]
        k = k_ref[rows, :]
        v = v_ref[rows, :]
        s = lax.dot_general(q, k, (((1,), (1,)), ((), ())), preferred_element_type=F32) * dmask
        intra = jnp.dot(s.astype(BF16), v, preferred_element_type=F32)
        cross = jnp.dot(q, state.astype(BF16), preferred_element_type=F32) * xi2
        kz = (k.astype(F32) * zeta2).astype(BF16)
        kv = lax.dot_general(kz, v, (((0,), (0,)), ((), ())), preferred_element_type=F32)
        state = state * gamma2 + kv
        y = _layernorm(intra + cross, gn)
        o_ref[rows, :] = (sg_ref[rows, :].astype(F32) * y).astype(o_ref.dtype)
    r_scr[...] = state


def retention(qk, v, sg, lg_tab, ret_gn, *, batch, rows=1024):
    t = qk.shape[0]
    seq = t // batch
    n_blk = seq // rows
    kern = functools.partial(_ret_kernel, n_chunks=rows // CHUNK)

    def col(off):
        return lambda b, h, c: (b * n_blk + c, off + h)

    blk = (rows, HEAD_DIM)
    return pl.pallas_call(
        kern,
        grid=(batch, N_HEADS, n_blk),
        in_specs=[pl.BlockSpec((1, 8, LANES), lambda b, h, c: (h, 0, 0)),
                  pl.BlockSpec(blk, col(0)),
                  pl.BlockSpec(blk, col(N_HEADS)),
                  pl.BlockSpec(blk, col(0)),
                  pl.BlockSpec(blk, col(0)),
                  pl.BlockSpec((1, HEAD_DIM), lambda b, h, c: (0, h))],
        out_specs=pl.BlockSpec(blk, col(0)),
        out_shape=jax.ShapeDtypeStruct((t, N_HEADS * HEAD_DIM), BF16),
        scratch_shapes=[pltpu.VMEM((HEAD_DIM, HEAD_DIM), F32)],
        compiler_params=_params(("parallel", "parallel", "arbitrary")),
        name="retention",
    )(lg_tab, qk, qk, v, sg, ret_gn)


def _sgu_kernel(u_ref, v_ref, w_ref, b_ref, o_ref, *, n_chunks):
    row = lax.broadcasted_iota(jnp.int32, (CHUNK, CHUNK), 0)
    col = lax.broadcasted_iota(jnp.int32, (CHUNK, CHUNK), 1)
    causal = row >= col
    gc = o_ref.shape[1] // N_GROUPS
    for g in range(N_GROUPS):
        w_m = jnp.where(causal, w_ref[g], 0.0).astype(BF16)
        bias = b_ref[g]
        cols = slice(g * gc, (g + 1) * gc)
        for c in range(n_chunks):
            rows = pl.ds(c * CHUNK, CHUNK)
            mixed = jnp.dot(w_m, v_ref[rows, cols], preferred_element_type=F32) + bias
            o_ref[rows, cols] = (u_ref[rows, cols].astype(F32) * mixed).astype(o_ref.dtype)


def spatial_gating(u, vn, sgu_w, sgu_b, layer, *, rows=512):
    t, d = u.shape
    kern = functools.partial(_sgu_kernel, n_chunks=rows // CHUNK)
    tile = pl.BlockSpec((rows, d), lambda i: (i, 0))
    return pl.pallas_call(
        kern,
        grid=(t // rows,),
        in_specs=[tile, tile,
                  pl.BlockSpec((None, N_GROUPS, CHUNK, CHUNK), lambda i: (layer, 0, 0, 0)),
                  pl.BlockSpec((None, N_GROUPS, CHUNK, 1), lambda i: (layer, 0, 0, 0))],
        out_specs=tile,
        out_shape=jax.ShapeDtypeStruct((t, d), BF16),
        compiler_params=_params(("parallel",)),
        name="spatial_gating",
    )(u, vn, sgu_w, sgu_b)


def _merge_kernel(x_ref, yr_ref, ys_ref, ga_ref, gb_ref, wr_ref, ws_ref, wo_ref, nxt_ref, o_ref, h2_ref):
    a = jnp.dot(yr_ref[...], wr_ref[...], preferred_element_type=F32)
    b = jnp.dot(ys_ref[...], ws_ref[...], preferred_element_type=F32)
    merged = (ga_ref[...].astype(F32) * a + gb_ref[...].astype(F32) * b).astype(BF16)
    xn = x_ref[...] + jnp.dot(merged, wo_ref[...], preferred_element_type=F32)
    o_ref[...] = xn
    h2_ref[...] = _rms(xn, nxt_ref[...]).astype(h2_ref.dtype)


def merge(x, y_ret, y_sgu, gates, w_ret, w_sgu, w_out, layer, nxt, *, tm=256):
    t, d = x.shape
    tile = lambda j: pl.BlockSpec((tm, d), lambda i: (i, j))
    weight = _resident((None, d, d), lambda i: (layer, 0, 0))
    return pl.pallas_call(
        _merge_kernel,
        grid=(t // tm,),
        in_specs=[tile(0), tile(0), tile(0), tile(0), tile(1), weight, weight, weight,
                  pl.BlockSpec((1, d), lambda i: (0, 0))],
        out_specs=[tile(0), tile(0)],
        out_shape=[jax.ShapeDtypeStruct((t, d), F32), jax.ShapeDtypeStruct((t, d), BF16)],
        compiler_params=_params(("parallel",)),
        name="merge",
    )(x, y_ret, y_sgu, gates, gates, w_ret, w_sgu, w_out, nxt)


def kernel(x, positions, ffn1_norm, ffn1_w_gu, ffn1_w_down, mix_norm, w_in, b_gate, ret_gn,
           sgu_ln, sgu_w, sgu_b, w_branch_ret, w_branch_sgu, w_out, ffn2_norm, ffn2_w_gu,
           ffn2_w_down, final_norm):
    batch, seq, d = x.shape
    depth = ffn1_norm.shape[0]
    t = batch * seq
    half = HEAD_DIM // 2

    xs = x.reshape(t, d)
    pos = positions.reshape(t, 1).astype(F32)
    inv = (ROPE_BASE ** (-jnp.arange(half, dtype=F32) / half)).reshape(1, half)
    cos, sin = rope_tables(pos, inv)
    log_gamma = jnp.log1p(-jnp.exp2(-5.0 - jnp.arange(N_HEADS, dtype=F32)))
    lg_tab = jnp.broadcast_to(log_gamma[:, None, None], (N_HEADS, 8, LANES))

    w1_gu, w1_dn = ffn1_w_gu.astype(BF16), ffn1_w_down.astype(BF16)
    w2_gu, w2_dn = ffn2_w_gu.astype(BF16), ffn2_w_down.astype(BF16)
    w_in_b = w_in.astype(BF16)
    w_br, w_bs, w_o = w_branch_ret.astype(BF16), w_branch_sgu.astype(BF16), w_out.astype(BF16)
    sgu_b4 = sgu_b.reshape(depth, N_GROUPS, CHUNK, 1)
    row = lambda a: a.reshape(1, -1)
    tm_p = 512
    table = pl.BlockSpec((tm_p, half), lambda s, i: (i, 0))

    h = rmsnorm(xs, row(ffn1_norm[0]))
    out = None
    for l in range(depth):
        xs, h = ffn(xs, h, w1_gu, w1_dn, l, row(mix_norm[l]))
        qk = proj(h, w_in_b, l, 0, 2, "rotary", (cos, sin), (table, table), tm=tm_p)
        v = proj(h, w_in_b, l, 2, 1, "plain", tm=tm_p)
        sg = proj(h, w_in_b, l, 3, 1, "silu", tm=tm_p)
        u = proj(h, w_in_b, l, 4, 1, "gelu", tm=tm_p)
        vn = proj(h, w_in_b, l, 5, 1, "lngelu", (row(sgu_ln[l]),),
                  (pl.BlockSpec((1, d), lambda s, i: (0, 0)),), tm=tm_p)
        gates = proj(h, w_in_b, l, 6, 2, "gate", (row(b_gate[l]),),
                     (pl.BlockSpec((1, d), lambda s, i: (0, s)),), tm=tm_p)
        y_ret = retention(qk, v, sg, lg_tab, row(ret_gn[l]), batch=batch)
        y_sgu = spatial_gating(u, vn, sgu_w, sgu_b4, l)
        xs, h = merge(xs, y_ret, y_sgu, gates, w_br, w_bs, w_o, l, row(ffn2_norm[l]))
        if l + 1 < depth:
            xs, h = ffn(xs, h, w2_gu, w2_dn, l, row(ffn1_norm[l + 1]))
        else:
            (out,) = ffn(xs, h, w2_gu, w2_dn, l, row(final_norm), emit_x=False, h2_dtype=F32)
    return out.reshape(batch, seq, d)
```

```python
import functools

import jax
import jax.numpy as jnp
from jax import lax
from jax.experimental import pallas as pl
from jax.experimental.pallas import tpu as pltpu

F32 = jnp.float32
BF16 = jnp.bfloat16

EPS = 1e-6
ROPE_BASE = 10000.0
N_HEADS = 8
HEAD_DIM = 256
CHUNK = 128
N_GROUPS = 8
N_SEG = 8
LANES = 128
MXU_DIM = 256
VMEM_LIMIT = 58 * 1024 * 1024


def _params(sem):
    return pltpu.CompilerParams(dimension_semantics=sem, vmem_limit_bytes=VMEM_LIMIT)


def _rms(x, g):
    return x * lax.rsqrt(jnp.mean(x * x, axis=-1, keepdims=True) + EPS) * g


def _gelu(x):
    return 0.5 * x * (1.0 + lax.erf(x * (2.0 ** -0.5)))


def _layernorm(x, g):
    mu = jnp.mean(x, axis=-1, keepdims=True)
    xc = x - mu
    var = jnp.mean(xc * xc, axis=-1, keepdims=True)
    return xc * lax.rsqrt(var + EPS) * g


def _resident(shape, index_map):
    return pl.BlockSpec(shape, index_map, pipeline_mode=pl.Buffered(1))


def _rope_kernel(pos_ref, inv_ref, cos_ref, sin_ref):
    ang = pos_ref[...] * inv_ref[...]
    cos_ref[...] = jnp.cos(ang)
    sin_ref[...] = jnp.sin(ang)


def rope_tables(pos, inv, *, tr=1024):
    t = pos.shape[0]
    half = inv.shape[1]
    return pl.pallas_call(
        _rope_kernel,
        grid=(t // tr,),
        in_specs=[pl.BlockSpec((tr, 1), lambda i: (i, 0)),
                  pl.BlockSpec((1, half), lambda i: (0, 0))],
        out_specs=[pl.BlockSpec((tr, half), lambda i: (i, 0))] * 2,
        out_shape=[jax.ShapeDtypeStruct((t, half), F32)] * 2,
        compiler_params=_params(("parallel",)),
        name="rope_tables",
    )(pos, inv)


def _rms_kernel(x_ref, g_ref, o_ref):
    o_ref[...] = _rms(x_ref[...], g_ref[...]).astype(o_ref.dtype)


def rmsnorm(x, g, *, tm=512):
    t, d = x.shape
    return pl.pallas_call(
        _rms_kernel,
        grid=(t // tm,),
        in_specs=[pl.BlockSpec((tm, d), lambda i: (i, 0)),
                  pl.BlockSpec((1, d), lambda i: (0, 0))],
        out_specs=pl.BlockSpec((tm, d), lambda i: (i, 0)),
        out_shape=jax.ShapeDtypeStruct((t, d), BF16),
        compiler_params=_params(("parallel",)),
        name="rmsnorm",
    )(x, g)


def _ffn_kernel(*refs, emit_x, emit_h):
    x_ref, h_ref, wa_ref, wg_ref, wd_ref = refs[:5]
    rest = list(refs[5:])
    nxt_ref = rest.pop(0) if emit_h else None
    o_ref = rest.pop(0) if emit_x else None
    h2_ref = rest.pop(0) if emit_h else None

    h = h_ref[...]
    a = jnp.dot(h, wa_ref[...], preferred_element_type=F32)
    g = jnp.dot(h, wg_ref[...], preferred_element_type=F32)
    p = (0.5 * jax.nn.silu(g) * a).astype(BF16)
    acc = x_ref[...] + jnp.dot(p, wd_ref[...], preferred_element_type=F32)
    if emit_x:
        o_ref[...] = acc
    if emit_h:
        h2_ref[...] = _rms(acc, nxt_ref[...]).astype(h2_ref.dtype)


def ffn_pass(x, h, w_gu, w_down, layer, slab, n_slabs, nxt=None, *, emit_x=True, h2_dtype=BF16, tm=256):
    t, d = x.shape
    d_ff = w_down.shape[1]
    ts = d_ff // n_slabs
    assert ts % MXU_DIM == 0, "a d_ff slab should fill whole MXU tiles"
    emit_h = nxt is not None
    kern = functools.partial(_ffn_kernel, emit_x=emit_x, emit_h=emit_h)
    tile = pl.BlockSpec((tm, d), lambda i: (i, 0))
    in_specs = [tile, tile,
                _resident((None, d, ts), lambda i: (layer, 0, slab)),
                _resident((None, d, ts), lambda i: (layer, 0, n_slabs + slab)),
                _resident((None, ts, d), lambda i: (layer, slab, 0))]
    args = [x, h, w_gu, w_gu, w_down]
    out_specs, out_shape = [], []
    if emit_h:
        in_specs.append(pl.BlockSpec((1, d), lambda i: (0, 0)))
        args.append(nxt)
    if emit_x:
        out_specs.append(tile)
        out_shape.append(jax.ShapeDtypeStruct((t, d), F32))
    if emit_h:
        out_specs.append(tile)
        out_shape.append(jax.ShapeDtypeStruct((t, d), h2_dtype))
    return pl.pallas_call(
        kern,
        grid=(t // tm,),
        in_specs=in_specs,
        out_specs=out_specs,
        out_shape=out_shape,
        compiler_params=_params(("parallel",)),
        name="ffn_pass",
    )(*args)


def ffn(x, h, w_gu, w_down, layer, nxt, *, emit_x=True, h2_dtype=BF16, n_slabs=2):
    for slab in range(n_slabs - 1):
        (x,) = ffn_pass(x, h, w_gu, w_down, layer, slab, n_slabs)
    return ffn_pass(x, h, w_gu, w_down, layer, n_slabs - 1, n_slabs, nxt, emit_x=emit_x, h2_dtype=h2_dtype)


def _proj_kernel(*refs, kind, ncol):
    h_ref, w_ref = refs[:2]
    o_ref = refs[-1]
    extra = refs[2:-1]
    h = h_ref[...]
    width = o_ref.shape[1]
    cw = width // ncol

    def cols(n):
        return slice(n * cw, (n + 1) * cw)

    if kind == "lngelu":
        (ln_ref,) = extra
        total = jnp.zeros((h.shape[0], 1), F32)
        parts = []
        for n in range(ncol):
            gl = _gelu(jnp.dot(h, w_ref[:, cols(n)], preferred_element_type=F32))
            total = total + jnp.sum(gl, axis=-1, keepdims=True)
            parts.append(gl)
        mu = total * (1.0 / width)
        ssq = jnp.zeros_like(total)
        for n in range(ncol):
            parts[n] = parts[n] - mu
            ssq = ssq + jnp.sum(parts[n] * parts[n], axis=-1, keepdims=True)
        rstd = lax.rsqrt(ssq * (1.0 / width) + EPS)
        for n in range(ncol):
            o_ref[:, cols(n)] = (parts[n] * rstd * ln_ref[:, cols(n)]).astype(o_ref.dtype)
        return

    if kind == "rotary":
        cos_ref, sin_ref = extra
        cos, sin = cos_ref[...], sin_ref[...]
        scale = jnp.where(pl.program_id(0) == 0, 1.0, HEAD_DIM ** -0.5).astype(F32)
        cos, sin = cos * scale, sin * scale
        half = HEAD_DIM // 2
    for n in range(ncol):
        r = jnp.dot(h, w_ref[:, cols(n)], preferred_element_type=F32)
        if kind == "rotary":
            for hd in range(cw // HEAD_DIM):
                lo = n * cw + hd * HEAD_DIM
                x1 = r[:, hd * HEAD_DIM:hd * HEAD_DIM + half]
                x2 = r[:, hd * HEAD_DIM + half:(hd + 1) * HEAD_DIM]
                o_ref[:, lo:lo + half] = (x1 * cos - x2 * sin).astype(o_ref.dtype)
                o_ref[:, lo + half:lo + HEAD_DIM] = (x1 * sin + x2 * cos).astype(o_ref.dtype)
            continue
        if kind == "silu":
            r = jax.nn.silu(r)
        elif kind == "gelu":
            r = _gelu(r)
        elif kind == "gate":
            r = jax.nn.sigmoid(r + extra[0][:, cols(n)])
        o_ref[:, cols(n)] = r.astype(o_ref.dtype)


def proj(h, w_in, layer, seg0, nseg, kind, extra=(), extra_specs=(), *, tm=512, ncol=4):
    t, d = h.shape
    seg_w = w_in.shape[2] // N_SEG
    kern = functools.partial(_proj_kernel, kind=kind, ncol=ncol)
    return pl.pallas_call(
        kern,
        grid=(nseg, t // tm),
        in_specs=[pl.BlockSpec((tm, d), lambda s, i: (i, 0)),
                  pl.BlockSpec((None, d, seg_w), lambda s, i: (layer, 0, seg0 + s)),
                  *extra_specs],
        out_specs=pl.BlockSpec((tm, seg_w), lambda s, i: (i, s)),
        out_shape=jax.ShapeDtypeStruct((t, nseg * seg_w), BF16),
        compiler_params=_params(("arbitrary", "arbitrary")),
        name="proj_" + kind,
    )(h, w_in, *extra)


def _ret_kernel(lg_ref, q_ref, k_ref, v_ref, sg_ref, gn_ref, o_ref, r_scr, *, n_chunks, n_hd):
    c_blk = pl.program_id(2)

    @pl.when(c_blk == 0)
    def _():
        r_scr[...] = jnp.zeros_like(r_scr)

    row = lax.broadcasted_iota(jnp.int32, (CHUNK, CHUNK), 0)
    col = lax.broadcasted_iota(jnp.int32, (CHUNK, CHUNK), 1)
    diff = (row - col).astype(F32)
    causal = row >= col
    idx = lax.broadcasted_iota(jnp.int32, (CHUNK, LANES), 0).astype(F32)

    for hd in range(n_hd):
        lg = lg_ref[hd][0:1, :]
        dmask = jnp.where(causal, jnp.exp(jnp.where(causal, diff, 0.0) * lg), 0.0)
        xi = jnp.exp((idx + 1.0) * lg)
        zeta = jnp.exp((CHUNK - 1.0 - idx) * lg)
        gamma_c = jnp.exp(CHUNK * lg)
        xi2 = jnp.concatenate([xi, xi], axis=1)
        zeta2 = jnp.concatenate([zeta, zeta], axis=1)
        gamma2 = jnp.concatenate([gamma_c, gamma_c], axis=1)
        cols = slice(hd * HEAD_DIM, (hd + 1) * HEAD_DIM)
        gn = gn_ref[:, cols]

        state = r_scr[hd]
        for c in range(n_chunks):
            rows = pl.ds(c * CHUNK, CHUNK)
            q = q_ref[rows, cols]
            k = k_ref[rows, cols]
            v = v_ref[rows, cols]
            s = lax.dot_general(q, k, (((1,), (1,)), ((), ())), preferred_element_type=F32) * dmask
            intra = jnp.dot(s.astype(BF16), v, preferred_element_type=F32)
            cross = jnp.dot(q, state.astype(BF16), preferred_element_type=F32) * xi2
            kz = (k.astype(F32) * zeta2).astype(BF16)
            kv = lax.dot_general(kz, v, (((0,), (0,)), ((), ())), preferred_element_type=F32)
            state = state * gamma2 + kv
            y = _layernorm(intra + cross, gn)
            o_ref[rows, cols] = (sg_ref[rows, cols].astype(F32) * y).astype(o_ref.dtype)
        r_scr[hd] = state


def retention(qk, v, sg, lg_tab, ret_gn, *, batch, rows=1024, n_hd=4):
    t = qk.shape[0]
    seq = t // batch
    n_blk = seq // rows
    n_hp = N_HEADS // n_hd
    kern = functools.partial(_ret_kernel, n_chunks=rows // CHUNK, n_hd=n_hd)

    def col(off):
        return lambda b, h, c: (b * n_blk + c, off + h)

    blk = (rows, n_hd * HEAD_DIM)
    return pl.pallas_call(
        kern,
        grid=(batch, n_hp, n_blk),
        in_specs=[pl.BlockSpec((n_hd, 8, LANES), lambda b, h, c: (h, 0, 0)),
                  pl.BlockSpec(blk, col(0)),
                  pl.BlockSpec(blk, col(n_hp)),
                  pl.BlockSpec(blk, col(0)),
                  pl.BlockSpec(blk, col(0)),
                  pl.BlockSpec((1, n_hd * HEAD_DIM), lambda b, h, c: (0, h))],
        out_specs=pl.BlockSpec(blk, col(0)),
        out_shape=jax.ShapeDtypeStruct((t, N_HEADS * HEAD_DIM), BF16),
        scratch_shapes=[pltpu.VMEM((n_hd, HEAD_DIM, HEAD_DIM), F32)],
        compiler_params=_params(("parallel", "parallel", "arbitrary")),
        name="retention",
    )(lg_tab, qk, qk, v, sg, ret_gn)


def _sgu_kernel(u_ref, v_ref, w_ref, b_ref, o_ref, *, n_chunks):
    row = lax.broadcasted_iota(jnp.int32, (CHUNK, CHUNK), 0)
    col = lax.broadcasted_iota(jnp.int32, (CHUNK, CHUNK), 1)
    causal = row >= col
    gc = o_ref.shape[1] // N_GROUPS
    for g in range(N_GROUPS):
        w_m = jnp.where(causal, w_ref[g], 0.0).astype(BF16)
        bias = b_ref[g]
        cols = slice(g * gc, (g + 1) * gc)
        for c in range(n_chunks):
            rows = pl.ds(c * CHUNK, CHUNK)
            mixed = jnp.dot(w_m, v_ref[rows, cols], preferred_element_type=F32) + bias
            o_ref[rows, cols] = (u_ref[rows, cols].astype(F32) * mixed).astype(o_ref.dtype)


def spatial_gating(u, vn, sgu_w, sgu_b, layer, *, rows=512):
    t, d = u.shape
    kern = functools.partial(_sgu_kernel, n_chunks=rows // CHUNK)
    tile = pl.BlockSpec((rows, d), lambda i: (i, 0))
    return pl.pallas_call(
        kern,
        grid=(t // rows,),
        in_specs=[tile, tile,
                  pl.BlockSpec((None, N_GROUPS, CHUNK, CHUNK), lambda i: (layer, 0, 0, 0)),
                  pl.BlockSpec((None, N_GROUPS, CHUNK, 1), lambda i: (layer, 0, 0, 0))],
        out_specs=tile,
        out_shape=jax.ShapeDtypeStruct((t, d), BF16),
        compiler_params=_params(("parallel",)),
        name="spatial_gating",
    )(u, vn, sgu_w, sgu_b)


def _merge_kernel(x_ref, yr_ref, ys_ref, ga_ref, gb_ref, wr_ref, ws_ref, wo_ref, nxt_ref, o_ref, h2_ref):
    a = jnp.dot(yr_ref[...], wr_ref[...], preferred_element_type=F32)
    b = jnp.dot(ys_ref[...], ws_ref[...], preferred_element_type=F32)
    merged = (ga_ref[...].astype(F32) * a + gb_ref[...].astype(F32) * b).astype(BF16)
    xn = x_ref[...] + jnp.dot(merged, wo_ref[...], preferred_element_type=F32)
    o_ref[...] = xn
    h2_ref[...] = _rms(xn, nxt_ref[...]).astype(h2_ref.dtype)


def merge(x, y_ret, y_sgu, gates, w_ret, w_sgu, w_out, layer, nxt, *, tm=256):
    t, d = x.shape
    tile = lambda j: pl.BlockSpec((tm, d), lambda i: (i, j))
    weight = _resident((None, d, d), lambda i: (layer, 0, 0))
    return pl.pallas_call(
        _merge_kernel,
        grid=(t // tm,),
        in_specs=[tile(0), tile(0), tile(0), tile(0), tile(1), weight, weight, weight,
                  pl.BlockSpec((1, d), lambda i: (0, 0))],
        out_specs=[tile(0), tile(0)],
        out_shape=[jax.ShapeDtypeStruct((t, d), F32), jax.ShapeDtypeStruct((t, d), BF16)],
        compiler_params=_params(("parallel",)),
        name="merge",
    )(x, y_ret, y_sgu, gates, gates, w_ret, w_sgu, w_out, nxt)


def kernel(x, positions, ffn1_norm, ffn1_w_gu, ffn1_w_down, mix_norm, w_in, b_gate, ret_gn,
           sgu_ln, sgu_w, sgu_b, w_branch_ret, w_branch_sgu, w_out, ffn2_norm, ffn2_w_gu,
           ffn2_w_down, final_norm):
    batch, seq, d = x.shape
    depth = ffn1_norm.shape[0]
    t = batch * seq
    half = HEAD_DIM // 2

    xs = x.reshape(t, d)
    pos = positions.reshape(t, 1).astype(F32)
    inv = (ROPE_BASE ** (-jnp.arange(half, dtype=F32) / half)).reshape(1, half)
    cos, sin = rope_tables(pos, inv)
    log_gamma = jnp.log1p(-jnp.exp2(-5.0 - jnp.arange(N_HEADS, dtype=F32)))
    lg_tab = jnp.broadcast_to(log_gamma[:, None, None], (N_HEADS, 8, LANES))

    w1_gu, w1_dn = ffn1_w_gu.astype(BF16), ffn1_w_down.astype(BF16)
    w2_gu, w2_dn = ffn2_w_gu.astype(BF16), ffn2_w_down.astype(BF16)
    w_in_b = w_in.astype(BF16)
    w_br, w_bs, w_o = w_branch_ret.astype(BF16), w_branch_sgu.astype(BF16), w_out.astype(BF16)
    sgu_b4 = sgu_b.reshape(depth, N_GROUPS, CHUNK, 1)
    row = lambda a: a.reshape(1, -1)
    tm_p = 1024
    table =pl.BlockSpec((tm_p, half), lambda s, i: (i, 0))

    h = rmsnorm(xs, row(ffn1_norm[0]))
    out = None
    for l in range(depth):
        xs, h = ffn(xs, h, w1_gu, w1_dn, l, row(mix_norm[l]))
        qk = proj(h, w_in_b, l, 0, 2, "rotary", (cos, sin), (table, table), tm=tm_p)
        v = proj(h, w_in_b, l, 2, 1, "plain", tm=tm_p)
        sg = proj(h, w_in_b, l, 3, 1, "silu", tm=tm_p)
        u = proj(h, w_in_b, l, 4, 1, "gelu", tm=tm_p)
        vn = proj(h, w_in_b, l, 5, 1, "lngelu", (row(sgu_ln[l]),),
                  (pl.BlockSpec((1, d), lambda s, i: (0, 0)),), tm=tm_p)
        gates = proj(h, w_in_b, l, 6, 2, "gate", (row(b_gate[l]),),
                     (pl.BlockSpec((1, d), lambda s, i: (0, s)),), tm=tm_p)
        y_ret = retention(qk, v, sg, lg_tab, row(ret_gn[l]), batch=batch)
        y_sgu = spatial_gating(u, vn, sgu_w, sgu_b4, l)
        xs, h = merge(xs, y_ret, y_sgu, gates, w_br, w_bs, w_o, l, row(ffn2_norm[l]))
        if l + 1 < depth:
            xs, h = ffn(xs, h, w2_gu, w2_dn, l, row(ffn1_norm[l + 1]))
        else:
            (out,) = ffn(xs, h, w2_gu, w2_dn, l, row(final_norm), emit_x=False, h2_dtype=F32)
    return out.reshape(batch, seq, d)
```

```python
import functools

import jax
import jax.numpy as jnp
from jax import lax
from jax.experimental import pallas as pl
from jax.experimental.pallas import tpu as pltpu

F32 = jnp.float32
BF16 = jnp.bfloat16

EPS = 1e-6
ROPE_BASE = 10000.0
N_HEADS = 8
HEAD_DIM = 256
CHUNK = 128
N_GROUPS = 8
N_SEG = 8
LANES = 128
MXU_DIM = 256
VMEM_LIMIT = 58 * 1024 * 1024


def _params(sem):
    return pltpu.CompilerParams(dimension_semantics=sem, vmem_limit_bytes=VMEM_LIMIT)


def _rms(x, g):
    return x * lax.rsqrt(jnp.mean(x * x, axis=-1, keepdims=True) + EPS) * g


def _gelu(x):
    return 0.5 * x * (1.0 + lax.erf(x * (2.0 ** -0.5)))


def _layernorm(x, g):
    mu = jnp.mean(x, axis=-1, keepdims=True)
    xc = x - mu
    var = jnp.mean(xc * xc, axis=-1, keepdims=True)
    return xc * lax.rsqrt(var + EPS) * g


def _resident(shape, index_map):
    return pl.BlockSpec(shape, index_map, pipeline_mode=pl.Buffered(1))


def _rope_kernel(pos_ref, inv_ref, cos_ref, sin_ref):
    ang = pos_ref[...] * inv_ref[...]
    cos_ref[...] = jnp.cos(ang)
    sin_ref[...] = jnp.sin(ang)


def rope_tables(pos, inv, *, tr=1024):
    t = pos.shape[0]
    half = inv.shape[1]
    return pl.pallas_call(
        _rope_kernel,
        grid=(t // tr,),
        in_specs=[pl.BlockSpec((tr, 1), lambda i: (i, 0)),
                  pl.BlockSpec((1, half), lambda i: (0, 0))],
        out_specs=[pl.BlockSpec((tr, half), lambda i: (i, 0))] * 2,
        out_shape=[jax.ShapeDtypeStruct((t, half), F32)] * 2,
        compiler_params=_params(("parallel",)),
        name="rope_tables",
    )(pos, inv)


def _rms_kernel(x_ref, g_ref, o_ref):
    o_ref[...] = _rms(x_ref[...], g_ref[...]).astype(o_ref.dtype)


def rmsnorm(x, g, *, tm=512):
    t, d = x.shape
    return pl.pallas_call(
        _rms_kernel,
        grid=(t // tm,),
        in_specs=[pl.BlockSpec((tm, d), lambda i: (i, 0)),
                  pl.BlockSpec((1, d), lambda i: (0, 0))],
        out_specs=pl.BlockSpec((tm, d), lambda i: (i, 0)),
        out_shape=jax.ShapeDtypeStruct((t, d), BF16),
        compiler_params=_params(("parallel",)),
        name="rmsnorm",
    )(x, g)


def _ffn_kernel(*refs, emit_x, emit_h):
    x_ref, h_ref, wa_ref, wg_ref, wd_ref = refs[:5]
    rest = list(refs[5:])
    nxt_ref = rest.pop(0) if emit_h else None
    o_ref = rest.pop(0) if emit_x else None
    h2_ref = rest.pop(0) if emit_h else None

    h = h_ref[...]
    a = jnp.dot(h, wa_ref[...], preferred_element_type=F32)
    g = jnp.dot(h, wg_ref[...], preferred_element_type=F32)
    p = (0.5 * jax.nn.silu(g) * a).astype(BF16)
    acc = x_ref[...] + jnp.dot(p, wd_ref[...], preferred_element_type=F32)
    if emit_x:
        o_ref[...] = acc
    if emit_h:
        h2_ref[...] = _rms(acc, nxt_ref[...]).astype(h2_ref.dtype)


def ffn_pass(x, h, w_gu, w_down, layer, slab, n_slabs, nxt=None, *, emit_x=True, h2_dtype=BF16, tm=256):
    t, d = x.shape
    d_ff = w_down.shape[1]
    ts = d_ff // n_slabs
    assert ts % MXU_DIM == 0, "a d_ff slab should fill whole MXU tiles"
    emit_h = nxt is not None
    kern = functools.partial(_ffn_kernel, emit_x=emit_x, emit_h=emit_h)
    tile = pl.BlockSpec((tm, d), lambda i: (i, 0))
    in_specs = [tile, tile,
                _resident((None, d, ts), lambda i: (layer, 0, slab)),
                _resident((None, d, ts), lambda i: (layer, 0, n_slabs + slab)),
                _resident((None, ts, d), lambda i: (layer, slab, 0))]
    args = [x, h, w_gu, w_gu, w_down]
    out_specs, out_shape = [], []
    if emit_h:
        in_specs.append(pl.BlockSpec((1, d), lambda i: (0, 0)))
        args.append(nxt)
    if emit_x:
        out_specs.append(tile)
        out_shape.append(jax.ShapeDtypeStruct((t, d), F32))
    if emit_h:
        out_specs.append(tile)
        out_shape.append(jax.ShapeDtypeStruct((t, d), h2_dtype))
    return pl.pallas_call(
        kern,
        grid=(t // tm,),
        in_specs=in_specs,
        out_specs=out_specs,
        out_shape=out_shape,
        compiler_params=_params(("parallel",)),
        name="ffn_pass",
    )(*args)


def ffn(x, h, w_gu, w_down, layer, nxt, *, emit_x=True, h2_dtype=BF16, n_slabs=2):
    for slab in range(n_slabs - 1):
        (x,) = ffn_pass(x, h, w_gu, w_down, layer, slab, n_slabs)
    return ffn_pass(x, h, w_gu, w_down, layer, n_slabs - 1, n_slabs, nxt, emit_x=emit_x, h2_dtype=h2_dtype)


def _proj_kernel(*refs, kind, ncol):
    h_ref, w_ref = refs[:2]
    o_ref = refs[-1]
    extra = refs[2:-1]
    h = h_ref[...]
    width = o_ref.shape[1]
    cw = width // ncol

    def cols(n):
        return slice(n * cw, (n + 1) * cw)

    if kind == "lngelu":
        (ln_ref,) = extra
        total = jnp.zeros((h.shape[0], 1), F32)
        parts = []
        for n in range(ncol):
            gl = _gelu(jnp.dot(h, w_ref[:, cols(n)], preferred_element_type=F32))
            total = total + jnp.sum(gl, axis=-1, keepdims=True)
            parts.append(gl)
        mu = total * (1.0 / width)
        ssq = jnp.zeros_like(total)
        for n in range(ncol):
            parts[n] = parts[n] - mu
            ssq = ssq + jnp.sum(parts[n] * parts[n], axis=-1, keepdims=True)
        rstd = lax.rsqrt(ssq * (1.0 / width) + EPS)
        for n in range(ncol):
            o_ref[:, cols(n)] = (parts[n] * rstd * ln_ref[:, cols(n)]).astype(o_ref.dtype)
        return

    if kind == "rotary":
        cos_ref, sin_ref = extra
        cos, sin = cos_ref[...], sin_ref[...]
        scale = jnp.where(pl.program_id(0) == 0, 1.0, HEAD_DIM ** -0.5).astype(F32)
        cos, sin = cos * scale, sin * scale
        half = HEAD_DIM // 2
    for n in range(ncol):
        r = jnp.dot(h, w_ref[:, cols(n)], preferred_element_type=F32)
        if kind == "rotary":
            for hd in range(cw // HEAD_DIM):
                lo = n * cw + hd * HEAD_DIM
                x1 = r[:, hd * HEAD_DIM:hd * HEAD_DIM + half]
                x2 = r[:, hd * HEAD_DIM + half:(hd + 1) * HEAD_DIM]
                o_ref[:, lo:lo + half] = (x1 * cos - x2 * sin).astype(o_ref.dtype)
                o_ref[:, lo + half:lo + HEAD_DIM] = (x1 * sin + x2 * cos).astype(o_ref.dtype)
            continue
        if kind == "silu":
            r = jax.nn.silu(r)
        elif kind == "gelu":
            r = _gelu(r)
        elif kind == "gate":
            r = jax.nn.sigmoid(r + extra[0][:, cols(n)])
        o_ref[:, cols(n)] = r.astype(o_ref.dtype)


def proj(h, w_in, layer, seg0, nseg, kind, extra=(), extra_specs=(), *, tm=512, ncol=4):
    t, d = h.shape
    seg_w = w_in.shape[2] // N_SEG
    kern = functools.partial(_proj_kernel, kind=kind, ncol=ncol)
    return pl.pallas_call(
        kern,
        grid=(nseg, t // tm),
        in_specs=[pl.BlockSpec((tm, d), lambda s, i: (i, 0)),
                  pl.BlockSpec((None, d, seg_w), lambda s, i: (layer, 0, seg0 + s)),
                  *extra_specs],
        out_specs=pl.BlockSpec((tm, seg_w), lambda s, i: (i, s)),
        out_shape=jax.ShapeDtypeStruct((t, nseg * seg_w), BF16),
        compiler_params=_params(("arbitrary", "arbitrary")),
        name="proj_" + kind,
    )(h, w_in, *extra)


def _ret_kernel(lg_ref, q_ref, k_ref, v_ref, sg_ref, gn_ref, o_ref, r_scr, *, n_chunks, n_hd):
    c_blk = pl.program_id(2)

    @pl.when(c_blk == 0)
    def _():
        r_scr[...] = jnp.zeros_like(r_scr)

    row = lax.broadcasted_iota(jnp.int32, (CHUNK, CHUNK), 0)
    col = lax.broadcasted_iota(jnp.int32, (CHUNK, CHUNK), 1)
    diff = (row - col).astype(F32)
    causal = row >= col
    idx = lax.broadcasted_iota(jnp.int32, (CHUNK, LANES), 0).astype(F32)

    for hd in range(n_hd):
        lg = lg_ref[hd][0:1, :]
        dmask = jnp.where(causal, jnp.exp(jnp.where(causal, diff, 0.0) * lg), 0.0)
        xi = jnp.exp((idx + 1.0) * lg)
        zeta = jnp.exp((CHUNK - 1.0 - idx) * lg)
        gamma_c = jnp.exp(CHUNK * lg)
        xi2 = jnp.concatenate([xi, xi], axis=1)
        zeta2 = jnp.concatenate([zeta, zeta], axis=1)
        gamma2 = jnp.concatenate([gamma_c, gamma_c], axis=1)
        cols = slice(hd * HEAD_DIM, (hd + 1) * HEAD_DIM)
        gn = gn_ref[:, cols]

        state = r_scr[hd]
        for c in range(n_chunks):
            rows = pl.ds(c * CHUNK, CHUNK)
            q = q_ref[rows, cols]
            k = k_ref[rows, cols]
            v = v_ref[rows, cols]
            s = lax.dot_general(q, k, (((1,), (1,)), ((), ())), preferred_element_type=F32) * dmask
            intra = jnp.dot(s.astype(BF16), v, preferred_element_type=F32)
            cross = jnp.dot(q, state.astype(BF16), preferred_element_type=F32) * xi2
            kz = (k.astype(F32) * zeta2).astype(BF16)
            kv = lax.dot_general(kz, v, (((0,), (0,)), ((), ())), preferred_element_type=F32)
            state = state * gamma2 + kv
            y = _layernorm(intra + cross, gn)
            o_ref[rows, cols] = (sg_ref[rows, cols].astype(F32) * y).astype(o_ref.dtype)
        r_scr[hd] = state


def retention(qk, v, sg, lg_tab, ret_gn, *, batch, rows=1024, n_hd=4):
    t = qk.shape[0]
    seq = t // batch
    n_blk = seq // rows
    n_hp = N_HEADS // n_hd
    kern = functools.partial(_ret_kernel, n_chunks=rows // CHUNK, n_hd=n_hd)

    def col(off):
        return lambda b, h, c: (b * n_blk + c, off + h)

    blk = (rows, n_hd * HEAD_DIM)
    return pl.pallas_call(
        kern,
        grid=(batch, n_hp, n_blk),
        in_specs=[pl.BlockSpec((n_hd, 8, LANES), lambda b, h, c: (h, 0, 0)),
                  pl.BlockSpec(blk, col(0)),
                  pl.BlockSpec(blk, col(n_hp)),
                  pl.BlockSpec(blk, col(0)),
                  pl.BlockSpec(blk, col(0)),
                  pl.BlockSpec((1, n_hd * HEAD_DIM), lambda b, h, c: (0, h))],
        out_specs=pl.BlockSpec(blk, col(0)),
        out_shape=jax.ShapeDtypeStruct((t, N_HEADS * HEAD_DIM), BF16),
        scratch_shapes=[pltpu.VMEM((n_hd, HEAD_DIM, HEAD_DIM), F32)],
        compiler_params=_params(("parallel", "parallel", "arbitrary")),
        name="retention",
    )(lg_tab, qk, qk, v, sg, ret_gn)


def _merge_kernel(x_ref, yr_ref, u_ref, v_ref, sw_ref, sb_ref, ga_ref, gb_ref, wr_ref, ws_ref, wo_ref,
                  nxt_ref, o_ref, h2_ref, ys_scr):
    row = lax.broadcasted_iota(jnp.int32, (CHUNK, CHUNK), 0)
    col = lax.broadcasted_iota(jnp.int32, (CHUNK, CHUNK), 1)
    causal = row >= col
    gc = ys_scr.shape[1] // N_GROUPS
    for g in range(N_GROUPS):
        w_m = jnp.where(causal, sw_ref[g], 0.0).astype(BF16)
        bias = sb_ref[g]
        cols = slice(g * gc, (g + 1) * gc)
        for c in range(ys_scr.shape[0] // CHUNK):
            rows = pl.ds(c * CHUNK, CHUNK)
            mixed = jnp.dot(w_m, v_ref[rows, cols], preferred_element_type=F32) + bias
            ys_scr[rows, cols] = (u_ref[rows, cols].astype(F32) * mixed).astype(BF16)

    a = jnp.dot(yr_ref[...], wr_ref[...], preferred_element_type=F32)
    b = jnp.dot(ys_scr[...], ws_ref[...], preferred_element_type=F32)
    merged = (ga_ref[...].astype(F32) * a + gb_ref[...].astype(F32) * b).astype(BF16)
    xn = x_ref[...] + jnp.dot(merged, wo_ref[...], preferred_element_type=F32)
    o_ref[...] = xn
    h2_ref[...] = _rms(xn, nxt_ref[...]).astype(h2_ref.dtype)


def merge(x, y_ret, u, vn, sgu_w, sgu_b, gates, w_ret, w_sgu, w_out, layer, nxt, *, tm=256):
    t, d = x.shape
    assert tm % CHUNK == 0
    tile = lambda j: pl.BlockSpec((tm, d), lambda i: (i, j))
    weight = _resident((None, d, d), lambda i: (layer, 0, 0))
    return pl.pallas_call(
        _merge_kernel,
        grid=(t // tm,),
        in_specs=[tile(0), tile(0), tile(0), tile(0),
                  _resident((None, N_GROUPS, CHUNK, CHUNK), lambda i: (layer, 0, 0, 0)),
                  _resident((None, N_GROUPS, CHUNK, 1), lambda i: (layer, 0, 0, 0)),
                  tile(0), tile(1), weight, weight, weight,
                  pl.BlockSpec((1, d), lambda i: (0, 0))],
        out_specs=[tile(0), tile(0)],
        out_shape=[jax.ShapeDtypeStruct((t, d), F32), jax.ShapeDtypeStruct((t, d), BF16)],
        scratch_shapes=[pltpu.VMEM((tm, d), BF16)],
        compiler_params=_params(("parallel",)),
        name="merge",
    )(x, y_ret, u, vn, sgu_w, sgu_b, gates, gates, w_ret, w_sgu, w_out, nxt)


def kernel(x, positions, ffn1_norm, ffn1_w_gu, ffn1_w_down, mix_norm, w_in, b_gate, ret_gn,
           sgu_ln, sgu_w, sgu_b, w_branch_ret, w_branch_sgu, w_out, ffn2_norm, ffn2_w_gu,
           ffn2_w_down, final_norm):
    batch, seq, d = x.shape
    depth = ffn1_norm.shape[0]
    t = batch * seq
    half = HEAD_DIM // 2

    xs = x.reshape(t, d)
    pos = positions.reshape(t, 1).astype(F32)
    inv = (ROPE_BASE ** (-jnp.arange(half, dtype=F32) / half)).reshape(1, half)
    cos, sin = rope_tables(pos, inv)
    log_gamma = jnp.log1p(-jnp.exp2(-5.0 - jnp.arange(N_HEADS, dtype=F32)))
    lg_tab = jnp.broadcast_to(log_gamma[:, None, None], (N_HEADS, 8, LANES))

    w1_gu, w1_dn = ffn1_w_gu.astype(BF16), ffn1_w_down.astype(BF16)
    w2_gu, w2_dn = ffn2_w_gu.astype(BF16), ffn2_w_down.astype(BF16)
    w_in_b = w_in.astype(BF16)
    w_br, w_bs, w_o = w_branch_ret.astype(BF16), w_branch_sgu.astype(BF16), w_out.astype(BF16)
    sgu_b4 = sgu_b.reshape(depth, N_GROUPS, CHUNK, 1)
    row = lambda a: a.reshape(1, -1)
    tm_p = 1024
    table =pl.BlockSpec((tm_p, half), lambda s, i: (i, 0))

    h = rmsnorm(xs, row(ffn1_norm[0]))
    out = None
    for l in range(depth):
        xs, h = ffn(xs, h, w1_gu, w1_dn, l, row(mix_norm[l]))
        qk = proj(h, w_in_b, l, 0, 2, "rotary", (cos, sin), (table, table), tm=tm_p)
        v = proj(h, w_in_b, l, 2, 1, "plain", tm=tm_p)
        sg = proj(h, w_in_b, l, 3, 1, "silu", tm=tm_p)
        u = proj(h, w_in_b, l, 4, 1, "gelu", tm=tm_p)
        vn = proj(h, w_in_b, l, 5, 1, "lngelu", (row(sgu_ln[l]),),
                  (pl.BlockSpec((1, d), lambda s, i: (0, 0)),), tm=tm_p)
        gates = proj(h, w_in_b, l, 6, 2, "gate", (row(b_gate[l]),),
                     (pl.BlockSpec((1, d), lambda s, i: (0, s)),), tm=tm_p)
        y_ret = retention(qk, v, sg, lg_tab, row(ret_gn[l]), batch=batch)
        xs, h = merge(xs, y_ret, u, vn, sgu_w, sgu_b4, gates, w_br, w_bs, w_o, l, row(ffn2_norm[l]))
        if l + 1 < depth:
            xs, h = ffn(xs, h, w2_gu, w2_dn, l, row(ffn1_norm[l + 1]))
        else:
            (out,) = ffn(xs, h, w2_gu, w2_dn, l, row(final_norm), emit_x=False, h2_dtype=F32)
    return out.reshape(batch, seq, d)
```
